```python
import jax, jax.numpy as jnp
from jax import lax
import numpy as np

D_MODEL = 2048
BATCH = 8
SEQ = 4096
DEPTH = 2

N_A = DEPTH // 2
N_B = DEPTH - N_A
N_HEADS = 16
HEAD_DIM = D_MODEL // N_HEADS
CONV_WIDTH = 3
D_FF = 4 * D_MODEL
BLOCK_Q = 128
NORM_EPS = 1e-6

kernel_name = "yoco_shortconv_stickbreaking_sandwich"


def rmsnorm(x, g):
    x32 = x.astype(jnp.float32)
    y = x32 * lax.rsqrt(jnp.mean(jnp.square(x32), axis=-1, keepdims=True) + NORM_EPS)
    return (y * g.astype(jnp.float32)).astype(x.dtype)


def short_conv_mixer(u, w_in, conv_w, w_out):
    proj = u @ w_in
    xin, gate_c, gate_b = jnp.split(proj, 3, axis=-1)
    v = gate_c * xin
    S = v.shape[1]
    vp = jnp.pad(v, ((0, 0), (CONV_WIDTH - 1, 0), (0, 0)))
    conv = sum(conv_w[k] * vp[:, k:k + S] for k in range(CONV_WIDTH))
    return (gate_b * conv) @ w_out


def _stick_breaking_block(q_blk, k_pre, v_pre, q_start):
    Q = q_blk.shape[1]
    L = k_pre.shape[1]
    z = jnp.einsum('bqhd,bkhd->bhqk', q_blk.astype(jnp.float32), k_pre.astype(jnp.float32)) * (HEAD_DIM ** -0.5)
    t_idx = q_start + jnp.arange(Q)[:, None]
    s_idx = jnp.arange(L)[None, :]
    causal = s_idx < t_idx
    log_beta = jax.nn.log_sigmoid(z)
    log_1m = jnp.where(causal, log_beta - z, 0.0)
    incl = lax.cumsum(log_1m, axis=3, reverse=True)
    excl = jnp.concatenate([incl[..., 1:], jnp.zeros_like(incl[..., :1])], axis=-1)
    a = jnp.where(causal, jnp.exp(log_beta + excl), 0.0)
    return jnp.einsum('bhqk,bkhd->bqhd', a.astype(v_pre.dtype), v_pre)


def stick_breaking_attention(q, k, v):
    S = q.shape[1]
    outs = []
    for i in range(S // BLOCK_Q):
        start = i * BLOCK_Q
        end = start + BLOCK_Q
        outs.append(_stick_breaking_block(q[:, start:end], k[:, :end], v[:, :end], start))
    return jnp.concatenate(outs, axis=1)


def stick_breaking_mixer(u, w_q, k_sh, v_sh, w_o):
    Bsz, S, _ = u.shape
    q = (u @ w_q).reshape(Bsz, S, N_HEADS, HEAD_DIM)
    o = stick_breaking_attention(q, k_sh, v_sh)
    return o.reshape(Bsz, S, D_MODEL) @ w_o


def sq_relu_mlp(u, w_up, w_down):
    return jnp.square(jax.nn.relu(u @ w_up)) @ w_down


def setup_inputs(seed: int = 0) -> dict:
    key = jax.random.key(seed)
    ks = jax.random.split(key, 16)
    D = D_MODEL
    f32 = jnp.float32

    def nrm(k, shape, fan_in):
        return jax.random.normal(k, shape, f32) * (fan_in ** -0.5)

    def gain(k, shape):
        return 1.0 + 0.02 * jax.random.normal(k, shape, f32)

    return {
        "x": jax.random.normal(ks[0], (BATCH, SEQ, D), f32),
        "a_w_in": nrm(ks[1], (N_A, D, 3 * D), D),
        "a_conv_w": nrm(ks[2], (N_A, CONV_WIDTH, D), CONV_WIDTH),
        "a_w_out": nrm(ks[3], (N_A, D, D), D),
        "kv_norm_g": gain(ks[4], (D,)),
        "w_kv": nrm(ks[5], (D, 2 * D), D),
        "b_w_q": nrm(ks[6], (N_B, D, D), D),
        "b_w_o": nrm(ks[7], (N_B, D, D), D),
        "mix_pre_g": gain(ks[8], (DEPTH, D)),
        "mix_post_g": gain(ks[9], (DEPTH, D)),
        "mlp_pre_g": gain(ks[10], (DEPTH, D)),
        "mlp_post_g": gain(ks[11], (DEPTH, D)),
        "mlp_w_up": nrm(ks[12], (DEPTH, D, D_FF), D),
        "mlp_w_down": nrm(ks[13], (DEPTH, D_FF, D), D_FF),
    }


def reference(x, a_w_in, a_conv_w, a_w_out, kv_norm_g, w_kv, b_w_q, b_w_o,
              mix_pre_g, mix_post_g, mlp_pre_g, mlp_post_g, mlp_w_up, mlp_w_down):
    Bsz, S, _ = x.shape
    h = x
    k_sh = None
    v_sh = None
    for i in range(DEPTH):
        u = rmsnorm(h, mix_pre_g[i])
        if i < N_A:
            m = short_conv_mixer(u, a_w_in[i], a_conv_w[i], a_w_out[i])
        else:
            j = i - N_A
            m = stick_breaking_mixer(u, b_w_q[j], k_sh, v_sh, b_w_o[j])
        h = h + rmsnorm(m, mix_post_g[i])
        u = rmsnorm(h, mlp_pre_g[i])
        h = h + rmsnorm(sq_relu_mlp(u, mlp_w_up[i], mlp_w_down[i]), mlp_post_g[i])
        if i == N_A - 1:
            kv = rmsnorm(h, kv_norm_g) @ w_kv
            k_flat, v_flat = jnp.split(kv, 2, axis=-1)
            k_sh = k_flat.reshape(Bsz, S, N_HEADS, HEAD_DIM)
            v_sh = v_flat.reshape(Bsz, S, N_HEADS, HEAD_DIM)
    return h
```

```python
import functools

import jax
import jax.numpy as jnp
from jax import lax
from jax.experimental import pallas as pl
from jax.experimental.pallas import tpu as pltpu

HEAD_DIM = 128
CONV_WIDTH = 3
NORM_EPS = 1e-6

V7X_LANES = 128
V7X_SUBLANES = 8
VMEM_LIMIT_BYTES = 56 * 1024 * 1024

ROW_TILE = 512
COL_TILE = 512
ATTN_Q_TILE = 256
ATTN_K_TILE = V7X_LANES

_F32 = jnp.float32
_BF16 = jnp.bfloat16


def _params(*semantics):
    return pltpu.CompilerParams(dimension_semantics=semantics,
                                vmem_limit_bytes=VMEM_LIMIT_BYTES)


def _normalize(x):
    ms = jnp.mean(x * x, axis=-1, keepdims=True)
    return x * lax.rsqrt(ms + NORM_EPS)


def _dot(a, b):
    return jnp.dot(a, b, preferred_element_type=_F32)


def _conv_mixer_kernel(h_ref, gpre_ref, wx_ref, wc_ref, wb_ref, cw_ref, wout_ref, gpost_ref,
                       o_ref, u_ref, acc_ref, tail_ref, *, tiles_per_seq):
    i = pl.program_id(0)
    c = pl.program_id(1)
    tm = h_ref.shape[0]

    @pl.when(c == 0)
    def _():
        u_ref[...] = (_normalize(h_ref[...]) * gpre_ref[...]).astype(_BF16)
        acc_ref[...] = jnp.zeros_like(acc_ref)

    @pl.when(i % tiles_per_seq == 0)
    def _():
        tail_ref[c] = jnp.zeros(tail_ref.shape[1:], _F32)

    u = u_ref[...]
    v = _dot(u, wc_ref[...]) * _dot(u, wx_ref[...])
    tail = tail_ref[c]
    prev1 = tail[V7X_SUBLANES - 1:V7X_SUBLANES, :]
    prev2 = tail[V7X_SUBLANES - 2:V7X_SUBLANES - 1, :]
    tail_ref[c] = v[tm - V7X_SUBLANES:, :]

    row = lax.broadcasted_iota(jnp.int32, v.shape, 0)
    v1 = jnp.where(row == 0, prev1, pltpu.roll(v, 1, axis=0))
    v2 = jnp.where(row == 0, prev2,
                   jnp.where(row == 1, prev1, pltpu.roll(v, 2, axis=0)))
    cw = cw_ref[...]
    conv = cw[0:1, :] * v2 + cw[1:2, :] * v1 + cw[2:3, :] * v
    y = _dot(u, wb_ref[...]) * conv
    acc_ref[...] += _dot(y.astype(_BF16), wout_ref[...])

    @pl.when(c == pl.num_programs(1) - 1)
    def _():
        o_ref[...] = h_ref[...] + _normalize(acc_ref[...]) * gpost_ref[...]


def _conv_mixer(h, g_pre, w_in, conv_w, w_out, g_post, *, seq_len):
    t, d = h.shape
    tm, tc = ROW_TILE, COL_TILE
    assert t % tm == 0 and d % tc == 0 and seq_len % tm == 0
    assert w_in.shape == (d, 3 * d) and w_out.shape == (d, d) and conv_w.shape == (CONV_WIDTH, d)
    nc = d // tc
    row_spec = pl.BlockSpec((tm, d), lambda i, c: (i, 0))
    gain_spec = pl.BlockSpec((1, d), lambda i, c: (0, 0))
    return pl.pallas_call(
        functools.partial(_conv_mixer_kernel, tiles_per_seq=seq_len // tm),
        grid=(t // tm, nc),
        in_specs=[
            row_spec,
            gain_spec,
            pl.BlockSpec((d, tc), lambda i, c: (0, c)),
            pl.BlockSpec((d, tc), lambda i, c: (0, c + nc)),
            pl.BlockSpec((d, tc), lambda i, c: (0, c + 2 * nc)),
            pl.BlockSpec((CONV_WIDTH, tc), lambda i, c: (0, c)),
            pl.BlockSpec((tc, d), lambda i, c: (c, 0)),
            gain_spec,
        ],
        out_specs=row_spec,
        out_shape=jax.ShapeDtypeStruct((t, d), _F32),
        scratch_shapes=[
            pltpu.VMEM((tm, d), _BF16),
            pltpu.VMEM((tm, d), _F32),
            pltpu.VMEM((nc, V7X_SUBLANES, tc), _F32),
        ],
        compiler_params=_params("arbitrary", "arbitrary"),
        name="conv_mixer",
    )(h, g_pre, w_in, w_in, w_in, conv_w, w_out, g_post)


def _mlp_kernel(h_ref, gpre_ref, wup_ref, wdn_ref, gpost_ref, o_ref, u_ref, acc_ref):
    f = pl.program_id(1)

    @pl.when(f == 0)
    def _():
        u_ref[...] = (_normalize(h_ref[...]) * gpre_ref[...]).astype(_BF16)
        acc_ref[...] = jnp.zeros_like(acc_ref)

    hid = jnp.maximum(_dot(u_ref[...], wup_ref[...]), 0.0)
    acc_ref[...] += _dot((hid * hid).astype(_BF16), wdn_ref[...])

    @pl.when(f == pl.num_programs(1) - 1)
    def _():
        o_ref[...] = h_ref[...] + _normalize(acc_ref[...]) * gpost_ref[...]


def _mlp(h, g_pre, w_up, w_down, g_post):
    t, d = h.shape
    d_ff = w_up.shape[1]
    tm, tf = ROW_TILE, COL_TILE
    assert t % tm == 0 and d_ff % tf == 0
    assert w_up.shape == (d, d_ff) and w_down.shape == (d_ff, d)
    row_spec = pl.BlockSpec((tm, d), lambda i, f: (i, 0))
    gain_spec = pl.BlockSpec((1, d), lambda i, f: (0, 0))
    return pl.pallas_call(
        _mlp_kernel,
        grid=(t // tm, d_ff // tf),
        in_specs=[
            row_spec,
            gain_spec,
            pl.BlockSpec((d, tf), lambda i, f: (0, f)),
            pl.BlockSpec((tf, d), lambda i, f: (f, 0)),
            gain_spec,
        ],
        out_specs=row_spec,
        out_shape=jax.ShapeDtypeStruct((t, d), _F32),
        scratch_shapes=[pltpu.VMEM((tm, d), _BF16), pltpu.VMEM((tm, d), _F32)],
        compiler_params=_params("arbitrary", "arbitrary"),
        name="sq_relu_mlp",
    )(h, g_pre, w_up, w_down, g_post)


def _qkv_kernel(h_ref, gq_ref, gkv_ref, w_ref, o_ref, uq_ref, ukv_ref, *, q_tiles):
    j = pl.program_id(1)

    @pl.when(j == 0)
    def _():
        xn = _normalize(h_ref[...])
        uq_ref[...] = (xn * gq_ref[...]).astype(_BF16)
        ukv_ref[...] = (xn * gkv_ref[...]).astype(_BF16)

    @pl.when(j < q_tiles)
    def _():
        o_ref[...] = (_dot(uq_ref[...], w_ref[...]) * (HEAD_DIM ** -0.5)).astype(_BF16)

    @pl.when(j >= q_tiles)
    def _():
        o_ref[...] = _dot(ukv_ref[...], w_ref[...]).astype(_BF16)


def _qkv_proj(h, g_q, g_kv, w_qkv):
    t, d = h.shape
    tm, tn = ROW_TILE, COL_TILE
    assert t % tm == 0 and d % tn == 0 and w_qkv.shape == (d, 3 * d)
    gain_spec = pl.BlockSpec((1, d), lambda i, j: (0, 0))
    return pl.pallas_call(
        functools.partial(_qkv_kernel, q_tiles=d // tn),
        grid=(t // tm, 3 * d // tn),
        in_specs=[
            pl.BlockSpec((tm, d), lambda i, j: (i, 0)),
            gain_spec,
            gain_spec,
            pl.BlockSpec((d, tn), lambda i, j: (0, j)),
        ],
        out_specs=pl.BlockSpec((tm, tn), lambda i, j: (i, j)),
        out_shape=jax.ShapeDtypeStruct((t, 3 * d), _BF16),
        scratch_shapes=[pltpu.VMEM((tm, d), _BF16), pltpu.VMEM((tm, d), _BF16)],
        compiler_params=_params("arbitrary", "arbitrary"),
        name="qkv_proj",
    )(h, g_q, g_kv, w_qkv)


def _attn_kernel(q_ref, k_ref, v_ref, o_ref):
    seq = q_ref.shape[0]
    tq, tk = ATTN_Q_TILE, ATTN_K_TILE
    blocks_per_tile = tq // tk

    r = lax.broadcasted_iota(jnp.int32, (tk, 2 * tk), 0)
    s = lax.broadcasted_iota(jnp.int32, (tk, 2 * tk), 1)
    suffix = jnp.where((s >= tk) | (r > s), 1.0, 0.0).astype(_BF16)
    suffix2 = jnp.concatenate([suffix, suffix], axis=0)
    rel = (lax.broadcasted_iota(jnp.int32, (tq, tk), 0)
           - lax.broadcasted_iota(jnp.int32, (tq, tk), 1))

    def key_block(q, t0, kb, carry, acc, masked):
        s0 = pl.multiple_of(kb * tk, tk)
        z = lax.dot_general(q, k_ref[pl.ds(s0, tk), :], (((1,), (1,)), ((), ())),
                            preferred_element_type=_F32)
        log_beta = jnp.minimum(z, 0.0) - jnp.log(1.0 + jnp.exp(-jnp.abs(z)))
        log_1m = log_beta - z
        if masked:
            causal = rel > (s0 - t0)
            log_1m = jnp.where(causal, log_1m, 0.0)
        hi = log_1m.astype(_BF16)
        lo = (log_1m - hi.astype(_F32)).astype(_BF16)
        sums = _dot(jnp.concatenate([hi, lo], axis=1), suffix2)
        a = jnp.exp(log_beta + sums[:, :tk] + carry)
        if masked:
            a = jnp.where(causal, a, 0.0)
        acc = acc + _dot(a.astype(_BF16), v_ref[pl.ds(s0, tk), :])
        return carry + sums[:, tk:], acc

    def q_tile(qi, _):
        t0 = pl.multiple_of(qi * tq, tq)
        q = q_ref[pl.ds(t0, tq), :]
        carry = jnp.zeros((tq, tk), _F32)
        acc = jnp.zeros((tq, HEAD_DIM), _F32)
        first = qi * blocks_per_tile
        for d in reversed(range(blocks_per_tile)):
            carry, acc = key_block(q, t0, first + d, carry, acc, True)

        def full_blocks(it, state):
            carry, acc = state
            kb = first - 1 - it * blocks_per_tile
            for d in range(blocks_per_tile):
                carry, acc = key_block(q, t0, kb - d, carry, acc, False)
            return carry, acc

        carry, acc = lax.fori_loop(0, qi, full_blocks, (carry, acc))
        o_ref[pl.ds(t0, tq), :] = acc.astype(o_ref.dtype)
        return 0

    lax.fori_loop(0, seq // tq, q_tile, 0)


def _attention(qkv, *, batch, seq_len, d_model):
    n_heads = d_model // HEAD_DIM
    assert qkv.shape == (batch * seq_len, 3 * d_model)
    assert seq_len % ATTN_Q_TILE == 0 and ATTN_Q_TILE % ATTN_K_TILE == 0
    blk = (seq_len, HEAD_DIM)
    return pl.pallas_call(
        _attn_kernel,
        grid=(batch, n_heads),
        in_specs=[
            pl.BlockSpec(blk, lambda b, h: (b, h)),
            pl.BlockSpec(blk, lambda b, h: (b, h + n_heads)),
            pl.BlockSpec(blk, lambda b, h: (b, h + 2 * n_heads)),
        ],
        out_specs=pl.BlockSpec(blk, lambda b, h: (b, h)),
        out_shape=jax.ShapeDtypeStruct((batch * seq_len, d_model), _BF16),
        compiler_params=_params("arbitrary", "arbitrary"),
        name="stick_breaking_attention",
    )(qkv, qkv, qkv)


def _out_proj_kernel(h_ref, o_ref, w_ref, gpost_ref, out_ref, acc_ref):
    k = pl.program_id(1)

    @pl.when(k == 0)
    def _():
        acc_ref[...] = jnp.zeros_like(acc_ref)

    acc_ref[...] += _dot(o_ref[...], w_ref[...])

    @pl.when(k == pl.num_programs(1) - 1)
    def _():
        out_ref[...] = h_ref[...] + _normalize(acc_ref[...]) * gpost_ref[...]


def _out_proj(h, o, w_o, g_post):
    t, d = h.shape
    tm, tk = ROW_TILE, COL_TILE
    assert t % tm == 0 and d % tk == 0 and o.shape == (t, d) and w_o.shape == (d, d)
    row_spec = pl.BlockSpec((tm, d), lambda i, k: (i, 0))
    return pl.pallas_call(
        _out_proj_kernel,
        grid=(t // tm, d // tk),
        in_specs=[
            row_spec,
            pl.BlockSpec((tm, tk), lambda i, k: (i, k)),
            pl.BlockSpec((tk, d), lambda i, k: (k, 0)),
            pl.BlockSpec((1, d), lambda i, k: (0, 0)),
        ],
        out_specs=row_spec,
        out_shape=jax.ShapeDtypeStruct((t, d), _F32),
        scratch_shapes=[pltpu.VMEM((tm, d), _F32)],
        compiler_params=_params("arbitrary", "arbitrary"),
        name="attn_out_proj",
    )(h, o, w_o, g_post)


def kernel(x, a_w_in, a_conv_w, a_w_out, kv_norm_g, w_kv, b_w_q, b_w_o,
           mix_pre_g, mix_post_g, mlp_pre_g, mlp_post_g, mlp_w_up, mlp_w_down):
    batch, seq_len, d = x.shape
    depth = mix_pre_g.shape[0]
    n_conv = a_w_in.shape[0]
    assert d % HEAD_DIM == 0 and b_w_q.shape[0] == depth - n_conv

    def gain(g):
        return g.reshape(1, d).astype(_F32)

    h = x.reshape(batch * seq_len, d)
    qkv_shared = None
    for i in range(depth):
        if i < n_conv:
            h = _conv_mixer(h, gain(mix_pre_g[i]), a_w_in[i].astype(_BF16), a_conv_w[i],
                            a_w_out[i].astype(_BF16), gain(mix_post_g[i]), seq_len=seq_len)
        else:
            j = i - n_conv
            w_qkv = jnp.concatenate([b_w_q[j], w_kv], axis=1).astype(_BF16)
            qkv = _qkv_proj(h, gain(mix_pre_g[i]), gain(kv_norm_g), w_qkv)
            if qkv_shared is None:
                qkv_shared = qkv
            else:
                qkv = jnp.concatenate([qkv[:, :d], qkv_shared[:, d:]], axis=1)
            o = _attention(qkv, batch=batch, seq_len=seq_len, d_model=d)
            h = _out_proj(h, o, b_w_o[j].astype(_BF16), gain(mix_post_g[i]))
        h = _mlp(h, gain(mlp_pre_g[i]), mlp_w_up[i].astype(_BF16),
                 mlp_w_down[i].astype(_BF16), gain(mlp_post_g[i]))
    return h.reshape(batch, seq_len, d)
```

```python
import functools
import math

import numpy as np

import jax
import jax.numpy as jnp
from jax import lax
from jax.experimental import pallas as pl
from jax.experimental.pallas import tpu as pltpu

HEAD_DIM = 128
CONV_WIDTH = 3
NORM_EPS = 1e-6
LOG2E = math.log2(math.e)

V7X_LANES = 128
V7X_SUBLANES = 8
VMEM_LIMIT_BYTES = 56 * 1024 * 1024

ROW_TILE = 512
COL_TILE = 512
ATTN_TILE = 256
ATTN_ITEMS_PER_STEP = 2
MASKED_LOGIT = -1e30

_F32 = jnp.float32
_BF16 = jnp.bfloat16


def _params(*semantics):
    return pltpu.CompilerParams(dimension_semantics=semantics,
                                vmem_limit_bytes=VMEM_LIMIT_BYTES)


def _normalize(x):
    ms = jnp.mean(x * x, axis=-1, keepdims=True)
    return x * lax.rsqrt(ms + NORM_EPS)


def _dot(a, b):
    return jnp.dot(a, b, preferred_element_type=_F32)


def _conv_mixer_kernel(h_ref, gpre_ref, wx_ref, wc_ref, wb_ref, cw_ref, wout_ref, gpost_ref,
                       o_ref, u_ref, acc_ref, tail_ref, *, tiles_per_seq):
    i = pl.program_id(0)
    c = pl.program_id(1)
    tm = h_ref.shape[0]

    @pl.when(c == 0)
    def _():
        u_ref[...] = (_normalize(h_ref[...]) * gpre_ref[...]).astype(_BF16)
        acc_ref[...] = jnp.zeros_like(acc_ref)

    @pl.when(i % tiles_per_seq == 0)
    def _():
        tail_ref[c] = jnp.zeros(tail_ref.shape[1:], _F32)

    u = u_ref[...]
    v = _dot(u, wc_ref[...]) * _dot(u, wx_ref[...])
    tail = tail_ref[c]
    prev1 = tail[V7X_SUBLANES - 1:V7X_SUBLANES, :]
    prev2 = tail[V7X_SUBLANES - 2:V7X_SUBLANES - 1, :]
    tail_ref[c] = v[tm - V7X_SUBLANES:, :]

    row = lax.broadcasted_iota(jnp.int32, v.shape, 0)
    v1 = jnp.where(row == 0, prev1, pltpu.roll(v, 1, axis=0))
    v2 = jnp.where(row == 0, prev2,
                   jnp.where(row == 1, prev1, pltpu.roll(v, 2, axis=0)))
    cw = cw_ref[...]
    conv = cw[0:1, :] * v2 + cw[1:2, :] * v1 + cw[2:3, :] * v
    y = _dot(u, wb_ref[...]) * conv
    acc_ref[...] += _dot(y.astype(_BF16), wout_ref[...])

    @pl.when(c == pl.num_programs(1) - 1)
    def _():
        o_ref[...] = h_ref[...] + _normalize(acc_ref[...]) * gpost_ref[...]


def _conv_mixer(h, g_pre, w_in, conv_w, w_out, g_post, *, seq_len):
    t, d = h.shape
    tm, tc = ROW_TILE, COL_TILE
    assert t % tm == 0 and d % tc == 0 and seq_len % tm == 0
    assert w_in.shape == (d, 3 * d) and w_out.shape == (d, d) and conv_w.shape == (CONV_WIDTH, d)
    nc = d // tc
    row_spec = pl.BlockSpec((tm, d), lambda i, c: (i, 0))
    gain_spec = pl.BlockSpec((1, d), lambda i, c: (0, 0))
    return pl.pallas_call(
        functools.partial(_conv_mixer_kernel, tiles_per_seq=seq_len // tm),
        grid=(t // tm, nc),
        in_specs=[
            row_spec,
            gain_spec,
            pl.BlockSpec((d, tc), lambda i, c: (0, c)),
            pl.BlockSpec((d, tc), lambda i, c: (0, c + nc)),
            pl.BlockSpec((d, tc), lambda i, c: (0, c + 2 * nc)),
            pl.BlockSpec((CONV_WIDTH, tc), lambda i, c: (0, c)),
            pl.BlockSpec((tc, d), lambda i, c: (c, 0)),
            gain_spec,
        ],
        out_specs=row_spec,
        out_shape=jax.ShapeDtypeStruct((t, d), _F32),
        scratch_shapes=[
            pltpu.VMEM((tm, d), _BF16),
            pltpu.VMEM((tm, d), _F32),
            pltpu.VMEM((nc, V7X_SUBLANES, tc), _F32),
        ],
        compiler_params=_params("arbitrary", "arbitrary"),
        name="conv_mixer",
    )(h, g_pre, w_in, w_in, w_in, conv_w, w_out, g_post)


def _mlp_kernel(h_ref, gpre_ref, wup_ref, wdn_ref, gpost_ref, o_ref, u_ref, acc_ref):
    f = pl.program_id(1)

    @pl.when(f == 0)
    def _():
        u_ref[...] = (_normalize(h_ref[...]) * gpre_ref[...]).astype(_BF16)
        acc_ref[...] = jnp.zeros_like(acc_ref)

    hid = jnp.maximum(_dot(u_ref[...], wup_ref[...]), 0.0)
    acc_ref[...] += _dot((hid * hid).astype(_BF16), wdn_ref[...])

    @pl.when(f == pl.num_programs(1) - 1)
    def _():
        o_ref[...] = h_ref[...] + _normalize(acc_ref[...]) * gpost_ref[...]


def _mlp(h, g_pre, w_up, w_down, g_post):
    t, d = h.shape
    d_ff = w_up.shape[1]
    tm, tf = ROW_TILE, COL_TILE
    assert t % tm == 0 and d_ff % tf == 0
    assert w_up.shape == (d, d_ff) and w_down.shape == (d_ff, d)
    row_spec = pl.BlockSpec((tm, d), lambda i, f: (i, 0))
    gain_spec = pl.BlockSpec((1, d), lambda i, f: (0, 0))
    return pl.pallas_call(
        _mlp_kernel,
        grid=(t // tm, d_ff // tf),
        in_specs=[
            row_spec,
            gain_spec,
            pl.BlockSpec((d, tf), lambda i, f: (0, f)),
            pl.BlockSpec((tf, d), lambda i, f: (f, 0)),
            gain_spec,
        ],
        out_specs=row_spec,
        out_shape=jax.ShapeDtypeStruct((t, d), _F32),
        scratch_shapes=[pltpu.VMEM((tm, d), _BF16), pltpu.VMEM((tm, d), _F32)],
        compiler_params=_params("arbitrary", "arbitrary"),
        name="sq_relu_mlp",
    )(h, g_pre, w_up, w_down, g_post)


def _qkv_kernel(h_ref, gq_ref, gkv_ref, w_ref, o_ref, uq_ref, ukv_ref, *, q_tiles):
    j = pl.program_id(1)

    @pl.when(j == 0)
    def _():
        xn = _normalize(h_ref[...])
        uq_ref[...] = (xn * gq_ref[...]).astype(_BF16)
        ukv_ref[...] = (xn * gkv_ref[...]).astype(_BF16)

    @pl.when(j < q_tiles)
    def _():
        o_ref[...] = (_dot(uq_ref[...], w_ref[...]) * (LOG2E * HEAD_DIM ** -0.5)).astype(_BF16)

    @pl.when(j >= q_tiles)
    def _():
        o_ref[...] = _dot(ukv_ref[...], w_ref[...]).astype(_BF16)


def _qkv_proj(h, g_q, g_kv, w_qkv):
    t, d = h.shape
    tm, tn = ROW_TILE, COL_TILE
    assert t % tm == 0 and d % tn == 0 and w_qkv.shape == (d, 3 * d)
    gain_spec = pl.BlockSpec((1, d), lambda i, j: (0, 0))
    return pl.pallas_call(
        functools.partial(_qkv_kernel, q_tiles=d // tn),
        grid=(t // tm, 3 * d // tn),
        in_specs=[
            pl.BlockSpec((tm, d), lambda i, j: (i, 0)),
            gain_spec,
            gain_spec,
            pl.BlockSpec((d, tn), lambda i, j: (0, j)),
        ],
        out_specs=pl.BlockSpec((tm, tn), lambda i, j: (i, j)),
        out_shape=jax.ShapeDtypeStruct((t, 3 * d), _BF16),
        scratch_shapes=[pltpu.VMEM((tm, d), _BF16), pltpu.VMEM((tm, d), _BF16)],
        compiler_params=_params("arbitrary", "arbitrary"),
        name="qkv_proj",
    )(h, g_q, g_kv, w_qkv)


_PIPELINE_LAG = 3
_BIAS_NONE, _BIAS_DIAGONAL, _BIAS_ALL = 0, 1, 2


def _attn_items(n_tiles):
    per_step = ATTN_ITEMS_PER_STEP
    real = [(qi, ci, _BIAS_DIAGONAL if ci == qi else _BIAS_NONE, int(ci == qi))
            for qi in range(n_tiles) for ci in range(qi, -1, -1)]
    real += [(n_tiles - 1, 0, _BIAS_ALL, 0)] * (-len(real) % per_step)
    fill = [(0, 0, _BIAS_ALL, 1)] * (_PIPELINE_LAG * per_step)
    return np.array(fill + real + fill, dtype=np.int32).T, len(real) // per_step


def _attn_kernel(items_ref, q_ref, kt_ref, v_ref, o_ref,
                 z_ref, hilo_ref, zr_ref, arg_ref, carry_ref, acc_ref, bias_ref,
                 *, n_steps):
    tile = ATTN_TILE
    lanes = V7X_LANES
    per_step = ATTN_ITEMS_PER_STEP

    rr = lax.broadcasted_iota(jnp.int32, (2 * lanes, lanes), 0)
    ss = lax.broadcasted_iota(jnp.int32, (2 * lanes, lanes), 1)
    minus_later = jnp.where((rr & (lanes - 1)) >= ss, -1.0, 0.0).astype(_BF16)
    r = lax.broadcasted_iota(jnp.int32, (tile, tile), 0)
    s = lax.broadcasted_iota(jnp.int32, (tile, tile), 1)

    bias_ref[_BIAS_NONE] = jnp.zeros((tile, tile), _F32)
    bias_ref[_BIAS_DIAGONAL] = jnp.where(r > s, 0.0, MASKED_LOGIT)
    bias_ref[_BIAS_ALL] = jnp.full((tile, tile), MASKED_LOGIT, _F32)
    z_ref[...] = jnp.zeros_like(z_ref)
    hilo_ref[...] = jnp.zeros_like(hilo_ref)
    zr_ref[...] = jnp.full(zr_ref.shape, MASKED_LOGIT, _F32)
    arg_ref[...] = jnp.full(arg_ref.shape, MASKED_LOGIT, _F32)
    carry_ref[...] = jnp.zeros_like(carry_ref)
    acc_ref[...] = jnp.zeros_like(acc_ref)

    def scores_stage(it, m):
        t0 = pl.multiple_of(items_ref[0, it] * tile, tile)
        s0 = pl.multiple_of(items_ref[1, it] * tile, tile)
        z_ref[m] = _dot(q_ref[pl.ds(t0, tile), :], kt_ref[:, pl.ds(s0, tile)])

    def logs_stage(it, m):
        first = items_ref[3, it] == 1
        z = z_ref[m] + bias_ref[items_ref[2, it]]
        sign = jnp.uint32(0x80000000)
        neg_abs = lax.bitcast_convert_type(lax.bitcast_convert_type(z, jnp.uint32) | sign, _F32)
        sp = jnp.maximum(z, 0.0) + jnp.log(1.0 + jnp.exp2(neg_abs)) * LOG2E
        right = jnp.where(first, 0.0, carry_ref[...])
        for j in reversed(range(tile // lanes)):
            cols = slice(j * lanes, (j + 1) * lanes)
            blk = sp[:, cols]
            hi = blk.astype(_BF16)
            hilo_ref[m, j] = jnp.concatenate([hi, (blk - hi.astype(_F32)).astype(_BF16)], axis=1)
            zr_ref[m, :, cols] = z[:, cols] + right
            right = right - jnp.sum(blk, axis=-1, keepdims=True)
        carry_ref[...] = right

    def suffix_stage(m):
        for j in range(tile // lanes):
            cols = slice(j * lanes, (j + 1) * lanes)
            arg_ref[m, :, cols] = zr_ref[m, :, cols] + _dot(hilo_ref[m, j], minus_later)

    def values_stage(it, m):
        t0 = pl.multiple_of(items_ref[0, it] * tile, tile)
        s0 = pl.multiple_of(items_ref[1, it] * tile, tile)
        first = items_ref[3, it] == 1
        a = jnp.exp2(arg_ref[m]).astype(_BF16)
        acc = jnp.where(first, 0.0, acc_ref[...]) + _dot(a, v_ref[pl.ds(s0, tile), :])
        acc_ref[...] = acc
        o_ref[pl.ds(t0, tile), :] = acc.astype(o_ref.dtype)

    def step(k, _):
        for m in range(per_step):
            values_stage((k - 3) * per_step + m, m)
        for m in range(per_step):
            suffix_stage(m)
        for m in range(per_step):
            logs_stage((k - 1) * per_step + m, m)
        for m in range(per_step):
            scores_stage(k * per_step + m, m)
        return 0

    lax.fori_loop(_PIPELINE_LAG, n_steps + 2 * _PIPELINE_LAG, step, 0)


def _attention(qkv, *, batch, seq_len, d_model):
    n_heads = d_model // HEAD_DIM
    tile = ATTN_TILE
    per_step = ATTN_ITEMS_PER_STEP
    assert qkv.shape == (batch * seq_len, 3 * d_model)
    assert seq_len % tile == 0 and tile % V7X_LANES == 0
    items, n_steps = _attn_items(seq_len // tile)
    blk = (seq_len, HEAD_DIM)
    grid_spec = pltpu.PrefetchScalarGridSpec(
        num_scalar_prefetch=1,
        grid=(batch, n_heads),
        in_specs=[
            pl.BlockSpec(blk, lambda b, h, items: (b, h)),
            pl.BlockSpec((HEAD_DIM, seq_len), lambda b, h, items: (h, b)),
            pl.BlockSpec(blk, lambda b, h, items: (b, h + 2 * n_heads)),
        ],
        out_specs=pl.BlockSpec(blk, lambda b, h, items: (b, h)),
        scratch_shapes=[
            pltpu.VMEM((per_step, tile, tile), _F32),
            pltpu.VMEM((per_step, tile // V7X_LANES, tile, 2 * V7X_LANES), _BF16),
            pltpu.VMEM((per_step, tile, tile), _F32),
            pltpu.VMEM((per_step, tile, tile), _F32),
            pltpu.VMEM((tile, V7X_LANES), _F32),
            pltpu.VMEM((tile, HEAD_DIM), _F32),
            pltpu.VMEM((3, tile, tile), _F32),
        ],
    )
    return pl.pallas_call(
        functools.partial(_attn_kernel, n_steps=n_steps),
        grid_spec=grid_spec,
        out_shape=jax.ShapeDtypeStruct((batch * seq_len, d_model), _BF16),
        compiler_params=_params("arbitrary", "arbitrary"),
        name="stick_breaking_attention",
    )(jnp.asarray(items), qkv, qkv[:, d_model:2 * d_model].T, qkv)


def _out_proj_kernel(h_ref, o_ref, w_ref, gpost_ref, out_ref, acc_ref):
    k = pl.program_id(1)

    @pl.when(k == 0)
    def _():
        acc_ref[...] = jnp.zeros_like(acc_ref)

    acc_ref[...] += _dot(o_ref[...], w_ref[...])

    @pl.when(k == pl.num_programs(1) - 1)
    def _():
        out_ref[...] = h_ref[...] + _normalize(acc_ref[...]) * gpost_ref[...]


def _out_proj(h, o, w_o, g_post):
    t, d = h.shape
    tm, tk = ROW_TILE, COL_TILE
    assert t % tm == 0 and d % tk == 0 and o.shape == (t, d) and w_o.shape == (d, d)
    row_spec = pl.BlockSpec((tm, d), lambda i, k: (i, 0))
    return pl.pallas_call(
        _out_proj_kernel,
        grid=(t // tm, d // tk),
        in_specs=[
            row_spec,
            pl.BlockSpec((tm, tk), lambda i, k: (i, k)),
            pl.BlockSpec((tk, d), lambda i, k: (k, 0)),
            pl.BlockSpec((1, d), lambda i, k: (0, 0)),
        ],
        out_specs=row_spec,
        out_shape=jax.ShapeDtypeStruct((t, d), _F32),
        scratch_shapes=[pltpu.VMEM((tm, d), _F32)],
        compiler_params=_params("arbitrary", "arbitrary"),
        name="attn_out_proj",
    )(h, o, w_o, g_post)


def kernel(x, a_w_in, a_conv_w, a_w_out, kv_norm_g, w_kv, b_w_q, b_w_o,
           mix_pre_g, mix_post_g, mlp_pre_g, mlp_post_g, mlp_w_up, mlp_w_down):
    batch, seq_len, d = x.shape
    depth = mix_pre_g.shape[0]
    n_conv = a_w_in.shape[0]
    assert d % HEAD_DIM == 0 and b_w_q.shape[0] == depth - n_conv

    def gain(g):
        return g.reshape(1, d).astype(_F32)

    h = x.reshape(batch * seq_len, d)
    qkv_shared = None
    for i in range(depth):
        if i < n_conv:
            h = _conv_mixer(h, gain(mix_pre_g[i]), a_w_in[i].astype(_BF16), a_conv_w[i],
                            a_w_out[i].astype(_BF16), gain(mix_post_g[i]), seq_len=seq_len)
        else:
            j = i - n_conv
            w_qkv = jnp.concatenate([b_w_q[j], w_kv], axis=1).astype(_BF16)
            qkv = _qkv_proj(h, gain(mix_pre_g[i]), gain(kv_norm_g), w_qkv)
            if qkv_shared is None:
                qkv_shared = qkv
            else:
                qkv = jnp.concatenate([qkv[:, :d], qkv_shared[:, d:]], axis=1)
            o = _attention(qkv, batch=batch, seq_len=seq_len, d_model=d)
            h = _out_proj(h, o, b_w_o[j].astype(_BF16), gain(mix_post_g[i]))
        h = _mlp(h, gain(mlp_pre_g[i]), mlp_w_up[i].astype(_BF16),
                 mlp_w_down[i].astype(_BF16), gain(mlp_post_g[i]))
    return h.reshape(batch, seq_len, d)
```

```python
import functools
import math

import numpy as np

import jax
import jax.numpy as jnp
from jax import lax
from jax.experimental import pallas as pl
from jax.experimental.pallas import tpu as pltpu

HEAD_DIM = 128
CONV_WIDTH = 3
NORM_EPS = 1e-6
LOG2E = math.log2(math.e)

V7X_LANES = 128
V7X_SUBLANES = 8
VMEM_LIMIT_BYTES = 56 * 1024 * 1024

ROW_TILE = 512
COL_TILE = 512
FF_TILE = 1024
QKV_COL_TILE = 2048
ATTN_TILE = 256
ATTN_ITEMS_PER_STEP = 2
MASKED_LOGIT = -1e30

_F32 = jnp.float32
_BF16 = jnp.bfloat16


def _params(*semantics):
    return pltpu.CompilerParams(dimension_semantics=semantics,
                                vmem_limit_bytes=VMEM_LIMIT_BYTES)


def _normalize(x):
    ms = jnp.mean(x * x, axis=-1, keepdims=True)
    return x * lax.rsqrt(ms + NORM_EPS)


def _dot(a, b):
    return jnp.dot(a, b, preferred_element_type=_F32)


def _conv_mixer_kernel(h_ref, gpre_ref, wx_ref, wc_ref, wb_ref, cw_ref, wout_ref, gpost_ref,
                       o_ref, u_ref, acc_ref, tail_ref, *, tiles_per_seq):
    i = pl.program_id(0)
    c = pl.program_id(1)
    tm = h_ref.shape[0]

    @pl.when(c == 0)
    def _():
        u_ref[...] = (_normalize(h_ref[...]) * gpre_ref[...]).astype(_BF16)
        acc_ref[...] = jnp.zeros_like(acc_ref)

    @pl.when(i % tiles_per_seq == 0)
    def _():
        tail_ref[c] = jnp.zeros(tail_ref.shape[1:], _F32)

    u = u_ref[...]
    v = _dot(u, wc_ref[...]) * _dot(u, wx_ref[...])
    tail = tail_ref[c]
    prev1 = tail[V7X_SUBLANES - 1:V7X_SUBLANES, :]
    prev2 = tail[V7X_SUBLANES - 2:V7X_SUBLANES - 1, :]
    tail_ref[c] = v[tm - V7X_SUBLANES:, :]

    row = lax.broadcasted_iota(jnp.int32, v.shape, 0)
    v1 = jnp.where(row == 0, prev1, pltpu.roll(v, 1, axis=0))
    v2 = jnp.where(row == 0, prev2,
                   jnp.where(row == 1, prev1, pltpu.roll(v, 2, axis=0)))
    cw = cw_ref[...]
    conv = cw[0:1, :] * v2 + cw[1:2, :] * v1 + cw[2:3, :] * v
    y = _dot(u, wb_ref[...]) * conv
    acc_ref[...] += _dot(y.astype(_BF16), wout_ref[...])

    @pl.when(c == pl.num_programs(1) - 1)
    def _():
        o_ref[...] = h_ref[...] + _normalize(acc_ref[...]) * gpost_ref[...]


def _conv_mixer(h, g_pre, w_in, conv_w, w_out, g_post, *, seq_len):
    t, d = h.shape
    tm, tc = ROW_TILE, COL_TILE
    assert t % tm == 0 and d % tc == 0 and seq_len % tm == 0
    assert w_in.shape == (d, 3 * d) and w_out.shape == (d, d) and conv_w.shape == (CONV_WIDTH, d)
    nc = d // tc
    row_spec = pl.BlockSpec((tm, d), lambda i, c: (i, 0))
    gain_spec = pl.BlockSpec((1, d), lambda i, c: (0, 0))
    return pl.pallas_call(
        functools.partial(_conv_mixer_kernel, tiles_per_seq=seq_len // tm),
        grid=(t // tm, nc),
        in_specs=[
            row_spec,
            gain_spec,
            pl.BlockSpec((d, tc), lambda i, c: (0, c)),
            pl.BlockSpec((d, tc), lambda i, c: (0, c + nc)),
            pl.BlockSpec((d, tc), lambda i, c: (0, c + 2 * nc)),
            pl.BlockSpec((CONV_WIDTH, tc), lambda i, c: (0, c)),
            pl.BlockSpec((tc, d), lambda i, c: (c, 0)),
            gain_spec,
        ],
        out_specs=row_spec,
        out_shape=jax.ShapeDtypeStruct((t, d), _F32),
        scratch_shapes=[
            pltpu.VMEM((tm, d), _BF16),
            pltpu.VMEM((tm, d), _F32),
            pltpu.VMEM((nc, V7X_SUBLANES, tc), _F32),
        ],
        compiler_params=_params("arbitrary", "arbitrary"),
        name="conv_mixer",
    )(h, g_pre, w_in, w_in, w_in, conv_w, w_out, g_post)


def _mlp_kernel(h_ref, gpre_ref, wup_ref, wdn_ref, gpost_ref, o_ref, u_ref, acc_ref):
    f = pl.program_id(1)

    @pl.when(f == 0)
    def _():
        u_ref[...] = (_normalize(h_ref[...]) * gpre_ref[...]).astype(_BF16)
        acc_ref[...] = jnp.zeros_like(acc_ref)

    hid = jnp.maximum(_dot(u_ref[...], wup_ref[...]), 0.0)
    acc_ref[...] += _dot((hid * hid).astype(_BF16), wdn_ref[...])

    @pl.when(f == pl.num_programs(1) - 1)
    def _():
        o_ref[...] = h_ref[...] + _normalize(acc_ref[...]) * gpost_ref[...]


def _mlp(h, g_pre, w_up, w_down, g_post):
    t, d = h.shape
    d_ff = w_up.shape[1]
    tm, tf = ROW_TILE, min(FF_TILE, d_ff)
    assert t % tm == 0 and d_ff % tf == 0
    assert w_up.shape == (d, d_ff) and w_down.shape == (d_ff, d)
    row_spec = pl.BlockSpec((tm, d), lambda i, f: (i, 0))
    gain_spec = pl.BlockSpec((1, d), lambda i, f: (0, 0))
    return pl.pallas_call(
        _mlp_kernel,
        grid=(t // tm, d_ff // tf),
        in_specs=[
            row_spec,
            gain_spec,
            pl.BlockSpec((d, tf), lambda i, f: (0, f)),
            pl.BlockSpec((tf, d), lambda i, f: (f, 0)),
            gain_spec,
        ],
        out_specs=row_spec,
        out_shape=jax.ShapeDtypeStruct((t, d), _F32),
        scratch_shapes=[pltpu.VMEM((tm, d), _BF16), pltpu.VMEM((tm, d), _F32)],
        compiler_params=_params("arbitrary", "arbitrary"),
        name="sq_relu_mlp",
    )(h, g_pre, w_up, w_down, g_post)


def _qkv_kernel(h_ref, gq_ref, gkv_ref, w_ref, o_ref, uq_ref, ukv_ref, *, q_tiles):
    j = pl.program_id(1)

    @pl.when(j == 0)
    def _():
        xn = _normalize(h_ref[...])
        uq_ref[...] = (xn * gq_ref[...]).astype(_BF16)
        ukv_ref[...] = (xn * gkv_ref[...]).astype(_BF16)

    @pl.when(j < q_tiles)
    def _():
        o_ref[...] = (_dot(uq_ref[...], w_ref[...]) * (LOG2E * HEAD_DIM ** -0.5)).astype(_BF16)

    @pl.when(j >= q_tiles)
    def _():
        o_ref[...] = _dot(ukv_ref[...], w_ref[...]).astype(_BF16)


def _qkv_proj(h, g_q, g_kv, w_qkv):
    t, d = h.shape
    tm, tn = ROW_TILE, min(QKV_COL_TILE, d)
    assert t % tm == 0 and d % tn == 0 and w_qkv.shape == (d, 3 * d)
    gain_spec = pl.BlockSpec((1, d), lambda i, j: (0, 0))
    return pl.pallas_call(
        functools.partial(_qkv_kernel, q_tiles=d // tn),
        grid=(t // tm, 3 * d // tn),
        in_specs=[
            pl.BlockSpec((tm, d), lambda i, j: (i, 0)),
            gain_spec,
            gain_spec,
            pl.BlockSpec((d, tn), lambda i, j: (0, j)),
        ],
        out_specs=pl.BlockSpec((tm, tn), lambda i, j: (i, j)),
        out_shape=jax.ShapeDtypeStruct((t, 3 * d), _BF16),
        scratch_shapes=[pltpu.VMEM((tm, d), _BF16), pltpu.VMEM((tm, d), _BF16)],
        compiler_params=_params("arbitrary", "arbitrary"),
        name="qkv_proj",
    )(h, g_q, g_kv, w_qkv)


_PIPELINE_LAG = 3
_BIAS_NONE, _BIAS_DIAGONAL, _BIAS_ALL = 0, 1, 2


def _attn_items(n_tiles):
    per_step = ATTN_ITEMS_PER_STEP
    real = [(qi, ci, _BIAS_DIAGONAL if ci == qi else _BIAS_NONE, int(ci == qi))
            for qi in range(n_tiles) for ci in range(qi, -1, -1)]
    real += [(n_tiles - 1, 0, _BIAS_ALL, 0)] * (-len(real) % per_step)
    fill = [(0, 0, _BIAS_ALL, 1)] * (_PIPELINE_LAG * per_step)
    return np.array(fill + real + fill, dtype=np.int32).T, len(real) // per_step


def _attn_kernel(items_ref, q_ref, kt_ref, v_ref, o_ref,
                 z_ref, sp16_ref, zr_ref, arg_ref, carry_ref, acc_ref, bias_ref,
                 *, n_steps):
    tile = ATTN_TILE
    lanes = V7X_LANES
    per_step = ATTN_ITEMS_PER_STEP

    r = lax.broadcasted_iota(jnp.int32, (tile, tile), 0)
    s = lax.broadcasted_iota(jnp.int32, (tile, tile), 1)
    minus_later = jnp.where(r >= s, -1.0, 0.0).astype(_BF16)

    bias_ref[_BIAS_NONE] = jnp.zeros((tile, tile), _F32)
    bias_ref[_BIAS_DIAGONAL] = jnp.where(r > s, 0.0, MASKED_LOGIT)
    bias_ref[_BIAS_ALL] = jnp.full((tile, tile), MASKED_LOGIT, _F32)
    z_ref[...] = jnp.zeros_like(z_ref)
    sp16_ref[...] = jnp.zeros_like(sp16_ref)
    zr_ref[...] = jnp.full(zr_ref.shape, MASKED_LOGIT, _F32)
    arg_ref[...] = jnp.full(arg_ref.shape, MASKED_LOGIT, _F32)
    carry_ref[...] = jnp.zeros_like(carry_ref)
    acc_ref[...] = jnp.zeros_like(acc_ref)

    def scores_stage(it, m):
        t0 = pl.multiple_of(items_ref[0, it] * tile, tile)
        s0 = pl.multiple_of(items_ref[1, it] * tile, tile)
        z_ref[m] = _dot(q_ref[pl.ds(t0, tile), :], kt_ref[:, pl.ds(s0, tile)])

    def logs_stage(it, m):
        first = items_ref[3, it] == 1
        z = z_ref[m] + bias_ref[items_ref[2, it]]
        sign = jnp.uint32(0x80000000)
        neg_abs = lax.bitcast_convert_type(lax.bitcast_convert_type(z, jnp.uint32) | sign, _F32)
        sp = jnp.maximum(z, 0.0) + jnp.log(1.0 + jnp.exp2(neg_abs)) * LOG2E
        sp16_ref[m] = sp.astype(_BF16)
        right = jnp.where(first, 0.0, carry_ref[...])
        for j in range(tile // lanes):
            cols = slice(j * lanes, (j + 1) * lanes)
            zr_ref[m, :, cols] = z[:, cols] + right
        carry_ref[...] = right - jnp.sum(sp, axis=-1, keepdims=True)

    def suffix_stage(m):
        arg_ref[m] = zr_ref[m] + _dot(sp16_ref[m], minus_later)

    def values_stage(it, m):
        t0 = pl.multiple_of(items_ref[0, it] * tile, tile)
        s0 = pl.multiple_of(items_ref[1, it] * tile, tile)
        first = items_ref[3, it] == 1
        a = jnp.exp2(arg_ref[m]).astype(_BF16)
        acc = jnp.where(first, 0.0, acc_ref[...]) + _dot(a, v_ref[pl.ds(s0, tile), :])
        acc_ref[...] = acc
        o_ref[pl.ds(t0, tile), :] = acc.astype(o_ref.dtype)

    def step(k, _):
        for m in range(per_step):
            values_stage((k - 3) * per_step + m, m)
        for m in range(per_step):
            suffix_stage(m)
        for m in range(per_step):
            logs_stage((k - 1) * per_step + m, m)
        for m in range(per_step):
            scores_stage(k * per_step + m, m)
        return 0

    lax.fori_loop(_PIPELINE_LAG, n_steps + 2 * _PIPELINE_LAG, step, 0)


def _attention(qkv, *, batch, seq_len, d_model):
    n_heads = d_model // HEAD_DIM
    tile = ATTN_TILE
    per_step = ATTN_ITEMS_PER_STEP
    assert qkv.shape == (batch * seq_len, 3 * d_model)
    assert seq_len % tile == 0 and tile % V7X_LANES == 0
    items, n_steps = _attn_items(seq_len // tile)
    blk = (seq_len, HEAD_DIM)
    grid_spec = pltpu.PrefetchScalarGridSpec(
        num_scalar_prefetch=1,
        grid=(batch, n_heads),
        in_specs=[
            pl.BlockSpec(blk, lambda b, h, items: (b, h)),
            pl.BlockSpec((HEAD_DIM, seq_len), lambda b, h, items: (h, b)),
            pl.BlockSpec(blk, lambda b, h, items: (b, h + 2 * n_heads)),
        ],
        out_specs=pl.BlockSpec(blk, lambda b, h, items: (b, h)),
        scratch_shapes=[
            pltpu.VMEM((per_step, tile, tile), _F32),
            pltpu.VMEM((per_step, tile, tile), _BF16),
            pltpu.VMEM((per_step, tile, tile), _F32),
            pltpu.VMEM((per_step, tile, tile), _F32),
            pltpu.VMEM((tile, V7X_LANES), _F32),
            pltpu.VMEM((tile, HEAD_DIM), _F32),
            pltpu.VMEM((3, tile, tile), _F32),
        ],
    )
    return pl.pallas_call(
        functools.partial(_attn_kernel, n_steps=n_steps),
        grid_spec=grid_spec,
        out_shape=jax.ShapeDtypeStruct((batch * seq_len, d_model), _BF16),
        compiler_params=_params("arbitrary", "arbitrary"),
        name="stick_breaking_attention",
    )(jnp.asarray(items), qkv, qkv[:, d_model:2 * d_model].T, qkv)


def _out_proj_kernel(h_ref, o_ref, w_ref, gpost_ref, out_ref):
    out_ref[...] = h_ref[...] + _normalize(_dot(o_ref[...], w_ref[...])) * gpost_ref[...]


def _out_proj(h, o, w_o, g_post):
    t, d = h.shape
    tm = ROW_TILE
    assert t % tm == 0 and o.shape == (t, d) and w_o.shape == (d, d)
    row_spec = pl.BlockSpec((tm, d), lambda i: (i, 0))
    return pl.pallas_call(
        _out_proj_kernel,
        grid=(t // tm,),
        in_specs=[
            row_spec,
            row_spec,
            pl.BlockSpec((d, d), lambda i: (0, 0)),
            pl.BlockSpec((1, d), lambda i: (0, 0)),
        ],
        out_specs=row_spec,
        out_shape=jax.ShapeDtypeStruct((t, d), _F32),
        compiler_params=_params("arbitrary"),
        name="attn_out_proj",
    )(h, o, w_o, g_post)


def kernel(x, a_w_in, a_conv_w, a_w_out, kv_norm_g, w_kv, b_w_q, b_w_o,
           mix_pre_g, mix_post_g, mlp_pre_g, mlp_post_g, mlp_w_up, mlp_w_down):
    batch, seq_len, d = x.shape
    depth = mix_pre_g.shape[0]
    n_conv = a_w_in.shape[0]
    assert d % HEAD_DIM == 0 and b_w_q.shape[0] == depth - n_conv

    def gain(g):
        return g.reshape(1, d).astype(_F32)

    h = x.reshape(batch * seq_len, d)
    qkv_shared = None
    for i in range(depth):
        if i < n_conv:
            h = _conv_mixer(h, gain(mix_pre_g[i]), a_w_in[i].astype(_BF16), a_conv_w[i],
                            a_w_out[i].astype(_BF16), gain(mix_post_g[i]), seq_len=seq_len)
        else:
            j = i - n_conv
            w_qkv = jnp.concatenate([b_w_q[j], w_kv], axis=1).astype(_BF16)
            qkv = _qkv_proj(h, gain(mix_pre_g[i]), gain(kv_norm_g), w_qkv)
            if qkv_shared is None:
                qkv_shared = qkv
            else:
                qkv = jnp.concatenate([qkv[:, :d], qkv_shared[:, d:]], axis=1)
            o = _attention(qkv, batch=batch, seq_len=seq_len, d_model=d)
            h = _out_proj(h, o, b_w_o[j].astype(_BF16), gain(mix_post_g[i]))
        h = _mlp(h, gain(mlp_pre_g[i]), mlp_w_up[i].astype(_BF16),
                 mlp_w_down[i].astype(_BF16), gain(mlp_post_g[i]))
    return h.reshape(batch, seq_len, d)
```

```python
import functools
import math

import numpy as np

import jax
import jax.numpy as jnp
from jax import lax
from jax.experimental import pallas as pl
from jax.experimental.pallas import tpu as pltpu

HEAD_DIM = 128
CONV_WIDTH = 3
NORM_EPS = 1e-6
LOG2E = math.log2(math.e)

V7X_LANES = 128
V7X_SUBLANES = 8
VMEM_LIMIT_BYTES = 56 * 1024 * 1024

ROW_TILE = 512
COL_TILE = 512
FF_TILE = 1024
QKV_COL_TILE = 2048
ATTN_TILE = 256
ATTN_ITEMS_PER_STEP = 4
MASKED_LOGIT = -1e30

_F32 = jnp.float32
_BF16 = jnp.bfloat16


def _params(*semantics):
    return pltpu.CompilerParams(dimension_semantics=semantics,
                                vmem_limit_bytes=VMEM_LIMIT_BYTES)


def _normalize(x):
    ms = jnp.mean(x * x, axis=-1, keepdims=True)
    return x * lax.rsqrt(ms + NORM_EPS)


def _dot(a, b):
    return jnp.dot(a, b, preferred_element_type=_F32)


def _conv_mixer_kernel(h_ref, gpre_ref, wx_ref, wc_ref, wb_ref, cw_ref, wout_ref, gpost_ref,
                       o_ref, u_ref, acc_ref, tail_ref, *, tiles_per_seq):
    i = pl.program_id(0)
    c = pl.program_id(1)
    tm = h_ref.shape[0]

    @pl.when(c == 0)
    def _():
        u_ref[...] = (_normalize(h_ref[...]) * gpre_ref[...]).astype(_BF16)
        acc_ref[...] = jnp.zeros_like(acc_ref)

    @pl.when(i % tiles_per_seq == 0)
    def _():
        tail_ref[c] = jnp.zeros(tail_ref.shape[1:], _F32)

    u = u_ref[...]
    v = _dot(u, wc_ref[...]) * _dot(u, wx_ref[...])
    tail = tail_ref[c]
    prev1 = tail[V7X_SUBLANES - 1:V7X_SUBLANES, :]
    prev2 = tail[V7X_SUBLANES - 2:V7X_SUBLANES - 1, :]
    tail_ref[c] = v[tm - V7X_SUBLANES:, :]

    row = lax.broadcasted_iota(jnp.int32, v.shape, 0)
    v1 = jnp.where(row == 0, prev1, pltpu.roll(v, 1, axis=0))
    v2 = jnp.where(row == 0, prev2,
                   jnp.where(row == 1, prev1, pltpu.roll(v, 2, axis=0)))
    cw = cw_ref[...]
    conv = cw[0:1, :] * v2 + cw[1:2, :] * v1 + cw[2:3, :] * v
    y = _dot(u, wb_ref[...]) * conv
    acc_ref[...] += _dot(y.astype(_BF16), wout_ref[...])

    @pl.when(c == pl.num_programs(1) - 1)
    def _():
        o_ref[...] = h_ref[...] + _normalize(acc_ref[...]) * gpost_ref[...]


def _conv_mixer(h, g_pre, w_in, conv_w, w_out, g_post, *, seq_len):
    t, d = h.shape
    tm, tc = ROW_TILE, COL_TILE
    assert t % tm == 0 and d % tc == 0 and seq_len % tm == 0
    assert w_in.shape == (d, 3 * d) and w_out.shape == (d, d) and conv_w.shape == (CONV_WIDTH, d)
    nc = d // tc
    row_spec = pl.BlockSpec((tm, d), lambda i, c: (i, 0))
    gain_spec = pl.BlockSpec((1, d), lambda i, c: (0, 0))
    return pl.pallas_call(
        functools.partial(_conv_mixer_kernel, tiles_per_seq=seq_len // tm),
        grid=(t // tm, nc),
        in_specs=[
            row_spec,
            gain_spec,
            pl.BlockSpec((d, tc), lambda i, c: (0, c)),
            pl.BlockSpec((d, tc), lambda i, c: (0, c + nc)),
            pl.BlockSpec((d, tc), lambda i, c: (0, c + 2 * nc)),
            pl.BlockSpec((CONV_WIDTH, tc), lambda i, c: (0, c)),
            pl.BlockSpec((tc, d), lambda i, c: (c, 0)),
            gain_spec,
        ],
        out_specs=row_spec,
        out_shape=jax.ShapeDtypeStruct((t, d), _F32),
        scratch_shapes=[
            pltpu.VMEM((tm, d), _BF16),
            pltpu.VMEM((tm, d), _F32),
            pltpu.VMEM((nc, V7X_SUBLANES, tc), _F32),
        ],
        compiler_params=_params("arbitrary", "arbitrary"),
        name="conv_mixer",
    )(h, g_pre, w_in, w_in, w_in, conv_w, w_out, g_post)


def _mlp_kernel(h_ref, gpre_ref, wup_ref, wdn_ref, gpost_ref, o_ref, u_ref, acc_ref):
    f = pl.program_id(1)

    @pl.when(f == 0)
    def _():
        u_ref[...] = (_normalize(h_ref[...]) * gpre_ref[...]).astype(_BF16)
        acc_ref[...] = jnp.zeros_like(acc_ref)

    hid = jnp.maximum(_dot(u_ref[...], wup_ref[...]), 0.0)
    acc_ref[...] += _dot((hid * hid).astype(_BF16), wdn_ref[...])

    @pl.when(f == pl.num_programs(1) - 1)
    def _():
        o_ref[...] = h_ref[...] + _normalize(acc_ref[...]) * gpost_ref[...]


def _mlp(h, g_pre, w_up, w_down, g_post):
    t, d = h.shape
    d_ff = w_up.shape[1]
    tm, tf = ROW_TILE, min(FF_TILE, d_ff)
    assert t % tm == 0 and d_ff % tf == 0
    assert w_up.shape == (d, d_ff) and w_down.shape == (d_ff, d)
    row_spec = pl.BlockSpec((tm, d), lambda i, f: (i, 0))
    gain_spec = pl.BlockSpec((1, d), lambda i, f: (0, 0))
    return pl.pallas_call(
        _mlp_kernel,
        grid=(t // tm, d_ff // tf),
        in_specs=[
            row_spec,
            gain_spec,
            pl.BlockSpec((d, tf), lambda i, f: (0, f)),
            pl.BlockSpec((tf, d), lambda i, f: (f, 0)),
            gain_spec,
        ],
        out_specs=row_spec,
        out_shape=jax.ShapeDtypeStruct((t, d), _F32),
        scratch_shapes=[pltpu.VMEM((tm, d), _BF16), pltpu.VMEM((tm, d), _F32)],
        compiler_params=_params("arbitrary", "arbitrary"),
        name="sq_relu_mlp",
    )(h, g_pre, w_up, w_down, g_post)


def _qkv_kernel(h_ref, gq_ref, gkv_ref, w_ref, o_ref, uq_ref, ukv_ref, *, q_tiles):
    j = pl.program_id(1)

    @pl.when(j == 0)
    def _():
        xn = _normalize(h_ref[...])
        uq_ref[...] = (xn * gq_ref[...]).astype(_BF16)
        ukv_ref[...] = (xn * gkv_ref[...]).astype(_BF16)

    @pl.when(j < q_tiles)
    def _():
        o_ref[...] = (_dot(uq_ref[...], w_ref[...]) * (LOG2E * HEAD_DIM ** -0.5)).astype(_BF16)

    @pl.when(j >= q_tiles)
    def _():
        o_ref[...] = _dot(ukv_ref[...], w_ref[...]).astype(_BF16)


def _qkv_proj(h, g_q, g_kv, w_qkv):
    t, d = h.shape
    tm, tn = ROW_TILE, min(QKV_COL_TILE, d)
    assert t % tm == 0 and d % tn == 0 and w_qkv.shape == (d, 3 * d)
    gain_spec = pl.BlockSpec((1, d), lambda i, j: (0, 0))
    return pl.pallas_call(
        functools.partial(_qkv_kernel, q_tiles=d // tn),
        grid=(t // tm, 3 * d // tn),
        in_specs=[
            pl.BlockSpec((tm, d), lambda i, j: (i, 0)),
            gain_spec,
            gain_spec,
            pl.BlockSpec((d, tn), lambda i, j: (0, j)),
        ],
        out_specs=pl.BlockSpec((tm, tn), lambda i, j: (i, j)),
        out_shape=jax.ShapeDtypeStruct((t, 3 * d), _BF16),
        scratch_shapes=[pltpu.VMEM((tm, d), _BF16), pltpu.VMEM((tm, d), _BF16)],
        compiler_params=_params("arbitrary", "arbitrary"),
        name="qkv_proj",
    )(h, g_q, g_kv, w_qkv)


_PIPELINE_LAG = 3
_BIAS_NONE, _BIAS_DIAGONAL, _BIAS_ALL = 0, 1, 2


def _attn_items(n_tiles):
    per_step = ATTN_ITEMS_PER_STEP
    real = [(qi, ci, _BIAS_DIAGONAL if ci == qi else _BIAS_NONE, int(ci == qi))
            for qi in range(n_tiles) for ci in range(qi, -1, -1)]
    real += [(n_tiles - 1, 0, _BIAS_ALL, 0)] * (-len(real) % per_step)
    assert len(real) // per_step >= _PIPELINE_LAG
    return np.array(real, dtype=np.int32).T, len(real) // per_step


def _attn_kernel(items_ref, q_ref, kt_ref, v_ref, o_ref,
                 z_ref, sp16_ref, zr_ref, arg_ref, carry_ref, acc_ref, bias_ref,
                 *, n_steps):
    tile = ATTN_TILE
    lanes = V7X_LANES
    per_step = ATTN_ITEMS_PER_STEP

    r = lax.broadcasted_iota(jnp.int32, (tile, tile), 0)
    s = lax.broadcasted_iota(jnp.int32, (tile, tile), 1)
    minus_later = jnp.where(r >= s, -1.0, 0.0).astype(_BF16)

    bias_ref[_BIAS_NONE] = jnp.zeros((tile, tile), _F32)
    bias_ref[_BIAS_DIAGONAL] = jnp.where(r > s, 0.0, MASKED_LOGIT)
    bias_ref[_BIAS_ALL] = jnp.full((tile, tile), MASKED_LOGIT, _F32)
    carry_ref[...] = jnp.zeros_like(carry_ref)
    acc_ref[...] = jnp.zeros_like(acc_ref)

    def scores_stage(it, m):
        t0 = pl.multiple_of(items_ref[0, it] * tile, tile)
        s0 = pl.multiple_of(items_ref[1, it] * tile, tile)
        z_ref[m] = _dot(q_ref[pl.ds(t0, tile), :], kt_ref[:, pl.ds(s0, tile)])

    def logs_stage(it, m):
        first = items_ref[3, it] == 1
        z = z_ref[m] + bias_ref[items_ref[2, it]]
        sign = jnp.uint32(0x80000000)
        neg_abs = lax.bitcast_convert_type(lax.bitcast_convert_type(z, jnp.uint32) | sign, _F32)
        sp = jnp.maximum(z, 0.0) + jnp.log(1.0 + jnp.exp2(neg_abs)) * LOG2E
        sp16_ref[m] = sp.astype(_BF16)
        right = jnp.where(first, 0.0, carry_ref[...])
        for j in range(tile // lanes):
            cols = slice(j * lanes, (j + 1) * lanes)
            zr_ref[m, :, cols] = z[:, cols] + right
        carry_ref[...] = right - jnp.sum(sp, axis=-1, keepdims=True)

    def suffix_stage(m):
        arg_ref[m] = zr_ref[m] + _dot(sp16_ref[m], minus_later)

    def values_stage(it, m):
        t0 = pl.multiple_of(items_ref[0, it] * tile, tile)
        s0 = pl.multiple_of(items_ref[1, it] * tile, tile)
        first = items_ref[3, it] == 1
        a = jnp.exp2(arg_ref[m]).astype(_BF16)
        acc = jnp.where(first, 0.0, acc_ref[...]) + _dot(a, v_ref[pl.ds(s0, tile), :])
        acc_ref[...] = acc
        o_ref[pl.ds(t0, tile), :] = acc.astype(o_ref.dtype)

    def step(k, values=True, suffix=True, logs=True, scores=True):
        if values:
            for m in range(per_step):
                values_stage((k - 3) * per_step + m, m)
        if suffix:
            for m in range(per_step):
                suffix_stage(m)
        if logs:
            for m in range(per_step):
                logs_stage((k - 1) * per_step + m, m)
        if scores:
            for m in range(per_step):
                scores_stage(k * per_step + m, m)

    step(0, values=False, suffix=False, logs=False)
    step(1, values=False, suffix=False)
    step(2, values=False)
    lax.fori_loop(_PIPELINE_LAG, n_steps, lambda k, c: (step(k), c)[1], 0)
    step(n_steps, scores=False)
    step(n_steps + 1, logs=False, scores=False)
    step(n_steps + 2, suffix=False, logs=False, scores=False)


def _attention(qkv, *, batch, seq_len, d_model):
    n_heads = d_model // HEAD_DIM
    tile = ATTN_TILE
    per_step = ATTN_ITEMS_PER_STEP
    assert qkv.shape == (batch * seq_len, 3 * d_model)
    assert seq_len % tile == 0 and tile % V7X_LANES == 0
    items, n_steps = _attn_items(seq_len // tile)
    blk = (seq_len, HEAD_DIM)
    grid_spec = pltpu.PrefetchScalarGridSpec(
        num_scalar_prefetch=1,
        grid=(batch, n_heads),
        in_specs=[
            pl.BlockSpec(blk, lambda b, h, items: (b, h)),
            pl.BlockSpec((HEAD_DIM, seq_len), lambda b, h, items: (h, b)),
            pl.BlockSpec(blk, lambda b, h, items: (b, h + 2 * n_heads)),
        ],
        out_specs=pl.BlockSpec(blk, lambda b, h, items: (b, h)),
        scratch_shapes=[
            pltpu.VMEM((per_step, tile, tile), _F32),
            pltpu.VMEM((per_step, tile, tile), _BF16),
            pltpu.VMEM((per_step, tile, tile), _F32),
            pltpu.VMEM((per_step, tile, tile), _F32),
            pltpu.VMEM((tile, V7X_LANES), _F32),
            pltpu.VMEM((tile, HEAD_DIM), _F32),
            pltpu.VMEM((3, tile, tile), _F32),
        ],
    )
    return pl.pallas_call(
        functools.partial(_attn_kernel, n_steps=n_steps),
        grid_spec=grid_spec,
        out_shape=jax.ShapeDtypeStruct((batch * seq_len, d_model), _BF16),
        compiler_params=_params("arbitrary", "arbitrary"),
        name="stick_breaking_attention",
    )(jnp.asarray(items), qkv, qkv[:, d_model:2 * d_model].T, qkv)


def _out_proj_kernel(h_ref, o_ref, w_ref, gpost_ref, out_ref):
    out_ref[...] = h_ref[...] + _normalize(_dot(o_ref[...], w_ref[...])) * gpost_ref[...]


def _out_proj(h, o, w_o, g_post):
    t, d = h.shape
    tm = ROW_TILE
    assert t % tm == 0 and o.shape == (t, d) and w_o.shape == (d, d)
    row_spec = pl.BlockSpec((tm, d), lambda i: (i, 0))
    return pl.pallas_call(
        _out_proj_kernel,
        grid=(t // tm,),
        in_specs=[
            row_spec,
            row_spec,
            pl.BlockSpec((d, d), lambda i: (0, 0)),
            pl.BlockSpec((1, d), lambda i: (0, 0)),
        ],
        out_specs=row_spec,
        out_shape=jax.ShapeDtypeStruct((t, d), _F32),
        compiler_params=_params("arbitrary"),
        name="attn_out_proj",
    )(h, o, w_o, g_post)


def kernel(x, a_w_in, a_conv_w, a_w_out, kv_norm_g, w_kv, b_w_q, b_w_o,
           mix_pre_g, mix_post_g, mlp_pre_g, mlp_post_g, mlp_w_up, mlp_w_down):
    batch, seq_len, d = x.shape
    depth = mix_pre_g.shape[0]
    n_conv = a_w_in.shape[0]
    assert d % HEAD_DIM == 0 and b_w_q.shape[0] == depth - n_conv

    def gain(g):
        return g.reshape(1, d).astype(_F32)

    h = x.reshape(batch * seq_len, d)
    qkv_shared = None
    for i in range(depth):
        if i < n_conv:
            h = _conv_mixer(h, gain(mix_pre_g[i]), a_w_in[i].astype(_BF16), a_conv_w[i],
                            a_w_out[i].astype(_BF16), gain(mix_post_g[i]), seq_len=seq_len)
        else:
            j = i - n_conv
            w_qkv = jnp.concatenate([b_w_q[j], w_kv], axis=1).astype(_BF16)
            qkv = _qkv_proj(h, gain(mix_pre_g[i]), gain(kv_norm_g), w_qkv)
            if qkv_shared is None:
                qkv_shared = qkv
            else:
                qkv = jnp.concatenate([qkv[:, :d], qkv_shared[:, d:]], axis=1)
            o = _attention(qkv, batch=batch, seq_len=seq_len, d_model=d)
            h = _out_proj(h, o, b_w_o[j].astype(_BF16), gain(mix_post_g[i]))
        h = _mlp(h, gain(mlp_pre_g[i]), mlp_w_up[i].astype(_BF16),
                 mlp_w_down[i].astype(_BF16), gain(mlp_post_g[i]))
    return h.reshape(batch, seq_len, d)
```
